```python
import math
import jax
import jax.numpy as jnp
from jax import lax
import numpy as np

D_MODEL = 4096
BATCH = 4
SEQ = 2048
DEPTH = 2
DEC_BATCH = 8
DEC_SEQ = 1
PAST_LEN = 16384
PAGE_SIZE = 128

N_HEADS_A = 16
HEAD_DIM = 128
A_WIDTH = N_HEADS_A * HEAD_DIM
MOBA_BLOCK = 256
MOBA_TOPK = 3
Q_CHUNK = 16
N_BUCKETS = 32
MAX_DISTANCE = 128
N_HEADS_R = 16
R_DK = 128
R_DV = 128
R_WIDTH = N_HEADS_R * R_DK
R_VWIDTH = N_HEADS_R * R_DV
R_CHUNK = 64
PLE_DIM = 256
N_GROUPS = 4
EXPERTS_PER_GROUP = 4
N_EXPERTS = N_GROUPS * EXPERTS_PER_GROUP
D_EXPERT = 1024
MOE_TOPK = 2
EPS = 1e-6
NEG_BIG = -1e30
TINY = 1e-30

kernel_name = 'hybrid_moba_hgrn2_hmoe_decode_step'


def rmsnorm(x, g):
    x32 = x.astype(jnp.float32)
    y = x32 * lax.rsqrt(jnp.mean(x32 * x32, axis=-1, keepdims=True) + EPS)
    return (y * g.astype(jnp.float32)).astype(x.dtype)


def t5_bucket(rel):
    n = jnp.maximum(rel, 0)
    max_exact = N_BUCKETS // 2
    nf = jnp.maximum(n, 1).astype(jnp.float32)
    large = max_exact + (jnp.log(nf / max_exact) / math.log(MAX_DISTANCE / max_exact)
                         * (N_BUCKETS - max_exact)).astype(jnp.int32)
    large = jnp.clip(large, 0, N_BUCKETS - 1)
    return jnp.where(n < max_exact, n, large)


def moba_attend(q, k, v, q_pos, rel_bias):
    B, Lq, H, dh = q.shape
    T = k.shape[1]
    nblk = -(-T // MOBA_BLOCK)
    pad = nblk * MOBA_BLOCK - T
    if pad:
        k = jnp.pad(k, ((0, 0), (0, pad), (0, 0), (0, 0)))
        v = jnp.pad(v, ((0, 0), (0, pad), (0, 0), (0, 0)))
    kb = k.reshape(B, nblk, MOBA_BLOCK, H, dh)
    vb = v.reshape(B, nblk, MOBA_BLOCK, H, dh)
    kmean = jnp.mean(kb.astype(jnp.float32), axis=2)
    n_sel = min(MOBA_TOPK, nblk)
    qc = math.gcd(Lq, Q_CHUNK)
    n_chunks = Lq // qc
    q_chunks = q.reshape(B, n_chunks, qc, H, dh).transpose(1, 0, 3, 2, 4)
    pos_chunks = q_pos.reshape(n_chunks, qc)
    bi = jnp.arange(B)[:, None, None, None]
    hi = jnp.arange(H)[None, :, None, None]
    blk_ar = jnp.arange(nblk)
    in_blk = jnp.arange(MOBA_BLOCK)
    scale = dh ** -0.5

    def attend_chunk(args):
        qq, pp = args
        own = pp // MOBA_BLOCK
        own_b = jnp.broadcast_to(own[None, None, :, None], (B, H, qc, 1)).astype(jnp.int32)
        gate = jnp.einsum('bhqd,bnhd->bhqn', qq.astype(jnp.float32), kmean)
        gate = jnp.where(blk_ar[None, None, None, :] < own[None, None, :, None], gate, NEG_BIG)
        _, top_i = lax.top_k(gate, n_sel)
        top_i = top_i.astype(jnp.int32)
        blk_idx = jnp.concatenate([top_i, own_b], axis=-1)
        blk_ok = jnp.concatenate([top_i < own_b, jnp.ones((B, H, qc, 1), bool)], axis=-1)
        kg = kb[bi, blk_idx, :, hi]
        vg = vb[bi, blk_idx, :, hi]
        kpos = blk_idx[..., None] * MOBA_BLOCK + in_blk
        qp = pp[None, None, :, None, None]
        mask = blk_ok[..., None] & (kpos <= qp)
        bias = rel_bias[t5_bucket(qp - kpos), hi[..., None]].astype(jnp.float32)
        s = jnp.einsum('bhqd,bhqnkd->bhqnk', qq, kg, preferred_element_type=jnp.float32) * scale + bias
        s = jnp.where(mask, s, NEG_BIG)
        w = jax.nn.softmax(s.reshape(B, H, qc, -1), axis=-1).reshape(s.shape)
        return jnp.einsum('bhqnk,bhqnkd->bhqd', w.astype(vg.dtype), vg)

    out = lax.map(attend_chunk, (q_chunks, pos_chunks))
    return out.transpose(1, 0, 3, 2, 4).reshape(B, Lq, H, dh)


def hgrn2_scan(q, log_f, k, v, s0):
    B, L, H, DK = q.shape
    DV = v.shape[-1]
    C = min(R_CHUNK, L)
    n = -(-L // C)
    pad = n * C - L

    def blocks(a):
        a = jnp.pad(a, ((0, 0), (0, pad), (0, 0), (0, 0)))
        return a.reshape(B, n, C, H, a.shape[-1]).transpose(1, 0, 3, 2, 4)

    tri = jnp.tril(jnp.ones((C, C), bool))[:, :, None]

    def step(S, inp):
        qc, lf, kc, vc = inp
        b = jnp.cumsum(lf, axis=2)
        o = jnp.einsum('bhtk,bhkv->bhtv', qc * jnp.exp(b), S)
        decay = jnp.exp(jnp.where(tri, b[:, :, :, None, :] - b[:, :, None, :, :], NEG_BIG))
        a = jnp.einsum('bhtk,bhsk,bhtsk->bhts', qc, kc, decay)
        o = o + jnp.einsum('bhts,bhsv->bhtv', a, vc)
        b_end = b[:, :, -1:, :]
        S = jnp.exp(b_end[:, :, 0])[..., None] * S + jnp.einsum('bhsk,bhsv->bhkv', kc * jnp.exp(b_end - b), vc)
        return S, o

    s_fin, o = lax.scan(step, s0, (blocks(q), blocks(log_f), blocks(k), blocks(v)))
    o = o.transpose(1, 0, 3, 2, 4).reshape(B, n * C, H, DV)[:, :L]
    return o, s_fin


def hier_moe(h, w_grp, b_grp, w_erout, b_erout, w_gate, w_up, w_down):
    M = h.shape[0]
    g_logit = jnp.dot(h, w_grp, preferred_element_type=jnp.float32) + b_grp.astype(jnp.float32)
    g_val, g_idx = lax.top_k(jax.nn.softmax(g_logit, axis=-1), 1)
    e_logit = (jnp.dot(h, w_erout, preferred_element_type=jnp.float32)
               + b_erout.astype(jnp.float32)).reshape(M, N_GROUPS, EXPERTS_PER_GROUP)
    e_logit = jnp.take_along_axis(e_logit, g_idx[:, :, None], axis=1)[:, 0]
    e_val, e_idx = lax.top_k(jax.nn.softmax(e_logit, axis=-1), MOE_TOPK)
    wts = g_val * e_val / jnp.sum(e_val, axis=-1, keepdims=True)
    ids = g_idx * EXPERTS_PER_GROUP + e_idx
    combine = jnp.sum(jax.nn.one_hot(ids, N_EXPERTS, dtype=jnp.float32) * wts[..., None], axis=1)
    a = jnp.einsum('md,edf->mef', h, w_gate)
    u = jnp.einsum('md,edf->mef', h, w_up)
    hid = jax.nn.silu(a) * u * combine[:, :, None].astype(h.dtype)
    return jnp.einsum('mef,efd->md', hid, w_down)


def block_forward(x, p_l, q_pos, k_past, v_past, s0, lower_l, rel_bias,
                  g_mix_l, w_in_l, g_q_l, g_k_l, g_rout_l, w_ba_l, w_br_l, w_out_l,
                  g_ffn_l, w_grp_l, b_grp_l, w_erout_l, b_erout_l, w_gate_l, w_up_l, w_down_l,
                  g_ple_l, w_pg_l, w_pp_l):
    B, L, D = x.shape
    h = rmsnorm(x, g_mix_l)
    sizes = (A_WIDTH, A_WIDTH, A_WIDTH, R_WIDTH, R_WIDTH, R_VWIDTH, R_VWIDTH, D_MODEL, D_MODEL)
    cuts = [int(c) for c in np.cumsum(sizes)[:-1]]
    qa, ka, va, qr, fr, ir, gr, gate_a, gate_r = jnp.split(h @ w_in_l, cuts, axis=-1)
    qa = rmsnorm(qa.reshape(B, L, N_HEADS_A, HEAD_DIM), g_q_l)
    ka = rmsnorm(ka.reshape(B, L, N_HEADS_A, HEAD_DIM), g_k_l)
    va = va.reshape(B, L, N_HEADS_A, HEAD_DIM)
    if k_past is None:
        k_all, v_all = ka, va
    else:
        k_all = jnp.concatenate([k_past, ka.astype(k_past.dtype)], axis=1)
        v_all = jnp.concatenate([v_past, va.astype(v_past.dtype)], axis=1)
    o_a = moba_attend(qa, k_all, v_all, q_pos, rel_bias).reshape(B, L, A_WIDTH)
    zf = fr.astype(jnp.float32).reshape(B, L, N_HEADS_R, R_DK)
    lb = lower_l.reshape(N_HEADS_R, R_DK)
    f = lb + (1.0 - lb) * jax.nn.sigmoid(zf)
    log_f = jnp.log(jnp.maximum(f, TINY))
    k_r = (1.0 - lb) * jax.nn.sigmoid(-zf)
    o_r, s_new = hgrn2_scan(qr.astype(jnp.float32).reshape(B, L, N_HEADS_R, R_DK), log_f, k_r,
                            ir.astype(jnp.float32).reshape(B, L, N_HEADS_R, R_DV), s0)
    o_r = rmsnorm(o_r.astype(x.dtype), g_rout_l).reshape(B, L, R_VWIDTH) * jax.nn.silu(gr)
    merged = jax.nn.sigmoid(gate_a) * (o_a @ w_ba_l) + jax.nn.sigmoid(gate_r) * (o_r @ w_br_l)
    x = x + merged @ w_out_l
    y = hier_moe(rmsnorm(x, g_ffn_l).reshape(B * L, D), w_grp_l, b_grp_l, w_erout_l, b_erout_l,
                 w_gate_l, w_up_l, w_down_l)
    x = x + y.reshape(B, L, D)
    x = x + jax.nn.sigmoid(rmsnorm(x, g_ple_l) @ w_pg_l) * (p_l.astype(x.dtype) @ w_pp_l)
    return x, ka, va, s_new


def setup_inputs(seed: int = 0) -> dict:
    key = jax.random.key(seed)
    keys = list(jax.random.split(key, 32))

    def nrm(shape, scale):
        return jax.random.normal(keys.pop(), shape, jnp.float32) * scale

    def gain(shape):
        return 1.0 + nrm(shape, 0.01)

    n_pages = PAST_LEN // PAGE_SIZE
    n_pool = (DEC_BATCH * n_pages * 5) // 4
    in_width = 3 * A_WIDTH + 2 * R_WIDTH + 2 * R_VWIDTH + 2 * D_MODEL
    page_table = jax.random.permutation(keys.pop(), n_pool)[:DEC_BATCH * n_pages]
    page_table = page_table.reshape(DEC_BATCH, n_pages).astype(jnp.int32)
    return {
        'x_prompt': nrm((BATCH, SEQ, D_MODEL), 1.0),
        'x_sample': nrm((DEC_BATCH, DEC_SEQ, D_MODEL), 1.0),
        'cache_k': nrm((DEPTH, n_pool, PAGE_SIZE, N_HEADS_A, HEAD_DIM), 1.0),
        'cache_v': nrm((DEPTH, n_pool, PAGE_SIZE, N_HEADS_A, HEAD_DIM), 1.0),
        'state_hgrn': nrm((DEPTH, DEC_BATCH, N_HEADS_R, R_DK, R_DV), 0.1),
        'page_table': page_table,
        'p_prompt': nrm((DEPTH, BATCH, SEQ, PLE_DIM), 1.0),
        'p_sample': nrm((DEPTH, DEC_BATCH, DEC_SEQ, PLE_DIM), 1.0),
        'rel_bias': nrm((N_BUCKETS, N_HEADS_A), 0.5),
        'lb_logits': nrm((DEPTH, R_WIDTH), 0.5),
        'g_mix': gain((DEPTH, D_MODEL)),
        'w_in': nrm((DEPTH, D_MODEL, in_width), D_MODEL ** -0.5),
        'g_q': gain((DEPTH, HEAD_DIM)),
        'g_k': gain((DEPTH, HEAD_DIM)),
        'g_rout': gain((DEPTH, R_DV)),
        'w_ba': nrm((DEPTH, A_WIDTH, D_MODEL), A_WIDTH ** -0.5),
        'w_br': nrm((DEPTH, R_VWIDTH, D_MODEL), R_VWIDTH ** -0.5),
        'w_out': nrm((DEPTH, D_MODEL, D_MODEL), D_MODEL ** -0.5),
        'g_ffn': gain((DEPTH, D_MODEL)),
        'w_grp': nrm((DEPTH, D_MODEL, N_GROUPS), D_MODEL ** -0.5),
        'b_grp': nrm((DEPTH, N_GROUPS), 0.01),
        'w_erout': nrm((DEPTH, D_MODEL, N_EXPERTS), D_MODEL ** -0.5),
        'b_erout': nrm((DEPTH, N_EXPERTS), 0.01),
        'w_gate': nrm((DEPTH, N_EXPERTS, D_MODEL, D_EXPERT), D_MODEL ** -0.5),
        'w_up': nrm((DEPTH, N_EXPERTS, D_MODEL, D_EXPERT), D_MODEL ** -0.5),
        'w_down': nrm((DEPTH, N_EXPERTS, D_EXPERT, D_MODEL), D_EXPERT ** -0.5),
        'g_ple': gain((DEPTH, D_MODEL)),
        'w_pg': nrm((DEPTH, D_MODEL, D_MODEL), D_MODEL ** -0.5),
        'w_pp': nrm((DEPTH, PLE_DIM, D_MODEL), PLE_DIM ** -0.5),
    }


def reference(x_prompt, x_sample, cache_k, cache_v, state_hgrn, page_table, p_prompt, p_sample,
              rel_bias, lb_logits, g_mix, w_in, g_q, g_k, g_rout, w_ba, w_br, w_out, g_ffn,
              w_grp, b_grp, w_erout, b_erout, w_gate, w_up, w_down, g_ple, w_pg, w_pp):
    sm = jax.nn.softmax(lb_logits.astype(jnp.float32), axis=0)
    lower = jnp.cumsum(sm, axis=0) - sm[0:1]
    bp, lp = x_prompt.shape[0], x_prompt.shape[1]
    bd, ld = x_sample.shape[0], x_sample.shape[1]
    past = page_table.shape[1] * cache_k.shape[2]
    pos_p = jnp.arange(lp, dtype=jnp.int32)
    pos_s = past + jnp.arange(ld, dtype=jnp.int32)
    s0_p = jnp.zeros((bp, N_HEADS_R, R_DK, R_DV), jnp.float32)
    xp, xs = x_prompt, x_sample
    kps, vps, sps, kss, vss, sss = [], [], [], [], [], []
    for i in range(DEPTH):
        lw = (g_mix[i], w_in[i], g_q[i], g_k[i], g_rout[i], w_ba[i], w_br[i], w_out[i],
              g_ffn[i], w_grp[i], b_grp[i], w_erout[i], b_erout[i], w_gate[i], w_up[i], w_down[i],
              g_ple[i], w_pg[i], w_pp[i])
        xp, kp, vp, sp = block_forward(xp, p_prompt[i], pos_p, None, None, s0_p, lower[i], rel_bias, *lw)
        k_past = cache_k[i, page_table].reshape(bd, past, N_HEADS_A, HEAD_DIM)
        v_past = cache_v[i, page_table].reshape(bd, past, N_HEADS_A, HEAD_DIM)
        xs, ks, vs, ss = block_forward(xs, p_sample[i], pos_s, k_past, v_past,
                                       state_hgrn[i].astype(jnp.float32), lower[i], rel_bias, *lw)
        kps.append(kp); vps.append(vp); sps.append(sp)
        kss.append(ks); vss.append(vs); sss.append(ss)
    return (xp, xs, jnp.stack(kps), jnp.stack(vps), jnp.stack(sps), jnp.stack(kss), jnp.stack(vss), jnp.stack(sss))
```

```python
import functools
import math

import jax
import jax.numpy as jnp
from jax import lax
from jax.experimental import pallas as pl
from jax.experimental.pallas import tpu as pltpu

F32 = jnp.float32
BF16 = jnp.bfloat16
I32 = jnp.int32

MOBA_BLOCK = 256
MOBA_TOPK = 3
N_BUCKETS = 32
MAX_DISTANCE = 128
N_GROUPS = 4
EXPERTS_PER_GROUP = 4
N_EXPERTS = N_GROUPS * EXPERTS_PER_GROUP
MOE_TOPK = 2
EPS = 1e-6
NEG_BIG = -1e30
TINY = 1e-30

LANES = 128
BF16_ROWS = 16
VMEM_LIMIT = 56 * 1024 * 1024
HGRN_CHUNK = 32
MOE_ROWS = 256

_ARB = pltpu.ARBITRARY


def _cparams(n_axes):
    return pltpu.CompilerParams(dimension_semantics=(_ARB,) * n_axes,
                                vmem_limit_bytes=VMEM_LIMIT)


def _row_tile(m, target):
    best = None
    for t in range(BF16_ROWS, min(m, target) + 1, BF16_ROWS):
        if m % t == 0:
            best = t
    assert best is not None, (m, target)
    return best


def _sigmoid(x):
    return 1.0 / (1.0 + jnp.exp(-x))


def _silu(x):
    return x * _sigmoid(x)


def _dot(a, b):
    return jnp.dot(a, b, preferred_element_type=F32)


def _split_bf16(x):
    hi = x.astype(BF16)
    lo = (x - hi.astype(F32)).astype(BF16)
    return hi, lo


def _dot3(a, b):
    ah, al = _split_bf16(a)
    bh, bl = _split_bf16(b)
    return _dot(ah, bh) + (_dot(ah, bl) + _dot(al, bh))


def _rmsnorm_kernel(x_ref, g_ref, o_ref):
    x = x_ref[...]
    ms = jnp.mean(x * x, axis=-1, keepdims=True)
    o_ref[...] = (x * lax.rsqrt(ms + EPS) * g_ref[...]).astype(o_ref.dtype)


def _rmsnorm(x, g, out_dtype):
    m, d = x.shape
    tm = _row_tile(m, 320)
    return pl.pallas_call(
        _rmsnorm_kernel,
        grid=(m // tm,),
        in_specs=[pl.BlockSpec((tm, d), lambda i: (i, 0)),
                  pl.BlockSpec((1, d), lambda i: (0, 0))],
        out_specs=pl.BlockSpec((tm, d), lambda i: (i, 0)),
        out_shape=jax.ShapeDtypeStruct((m, d), out_dtype),
        compiler_params=_cparams(1),
        name="rmsnorm",
    )(x, g.reshape(1, d))


def _mm_kernel(*refs, n_x, n_extra, combine):
    xs = refs[:n_x]
    ws = refs[n_x:2 * n_x]
    extras = refs[2 * n_x:2 * n_x + n_extra]
    o_ref = refs[2 * n_x + n_extra]
    wbs = refs[2 * n_x + n_extra + 1:]

    @pl.when(pl.program_id(1) == 0)
    def _cast_weights():
        for w, wb in zip(ws, wbs):
            wb[...] = w[...].astype(BF16)

    accs = [_dot(x[...], wb[...]) for x, wb in zip(xs, wbs)]
    o_ref[...] = combine(accs, [e[...] for e in extras]).astype(o_ref.dtype)


def _fused_matmul(xs, ws, extras, combine, out_dtype, ncols, tn, name):
    m = xs[0].shape[0]
    tm = _row_tile(m, 1024)
    in_specs = []
    for x in xs:
        in_specs.append(pl.BlockSpec((tm, x.shape[1]), lambda j, i: (i, 0)))
    scratch = []
    for w, lead, col0 in ws:
        k = w.shape[-2]
        assert col0 % tn == 0 and ncols % tn == 0
        blk = (None,) * len(lead) + (k, tn)
        in_specs.append(pl.BlockSpec(
            blk, lambda j, i, lead=lead, c0=col0 // tn: lead + (0, c0 + j)))
        scratch.append(pltpu.VMEM((k, tn), BF16))
    for e, col0 in extras:
        assert col0 % tn == 0
        if e.shape[0] == 1:
            in_specs.append(pl.BlockSpec((1, tn), lambda j, i, c0=col0 // tn: (0, c0 + j)))
        else:
            in_specs.append(pl.BlockSpec((tm, tn), lambda j, i, c0=col0 // tn: (i, c0 + j)))
    kern = functools.partial(_mm_kernel, n_x=len(xs), n_extra=len(extras), combine=combine)
    return pl.pallas_call(
        kern,
        grid=(ncols // tn, m // tm),
        in_specs=in_specs,
        out_specs=pl.BlockSpec((tm, tn), lambda j, i: (i, j)),
        out_shape=jax.ShapeDtypeStruct((m, ncols), out_dtype),
        scratch_shapes=scratch,
        compiler_params=_cparams(2),
        name=name,
    )(*xs, *[w for w, _, _ in ws], *[e for e, _ in extras])


def _ep_plain(accs, ex):
    return accs[0]


def _ep_sigmoid(accs, ex):
    return _sigmoid(accs[0])


def _ep_residual(accs, ex):
    return ex[0] + accs[0]


def _ep_headnorm(accs, ex):
    acc, g = accs[0], ex[0]
    outs = []
    for c in range(acc.shape[1] // LANES):
        blk = acc[:, c * LANES:(c + 1) * LANES]
        ms = jnp.mean(blk * blk, axis=-1, keepdims=True)
        outs.append(blk * lax.rsqrt(ms + EPS) * g[:, c * LANES:(c + 1) * LANES])
    return jnp.concatenate(outs, axis=-1)


def _ep_merge(accs, ex):
    return ex[0].astype(F32) * accs[0] + ex[1].astype(F32) * accs[1]


def _ep_ple(accs, ex):
    return ex[0] + _sigmoid(accs[0]) * accs[1]


def _t5_bucket(rel):
    n = jnp.maximum(rel, 0)
    max_exact = N_BUCKETS // 2
    nf = jnp.maximum(n, 1).astype(F32)
    large = max_exact + (jnp.log(nf / max_exact) / math.log(MAX_DISTANCE / max_exact)
                         * (N_BUCKETS - max_exact)).astype(I32)
    large = jnp.clip(large, 0, N_BUCKETS - 1)
    return jnp.where(n < max_exact, n, large)


def _bias_kernel(rb_ref, idx_ref, o_ref):
    h = pl.program_id(0)
    idx = idx_ref[...]
    acc = jnp.zeros(idx.shape, F32)
    for bkt in range(N_BUCKETS):
        acc = jnp.where(idx == bkt, rb_ref[bkt, h], acc)
    o_ref[...] = acc


def _bias_tiles(rel_bias):
    n_heads = rel_bias.shape[1]
    kk = jnp.arange(MOBA_BLOCK, dtype=I32)[:, None]
    qq = jnp.arange(MOBA_BLOCK, dtype=I32)[None, :]
    idx = jnp.stack([_t5_bucket(qq - kk), _t5_bucket(MOBA_BLOCK + qq - kk)])
    return pl.pallas_call(
        _bias_kernel,
        grid=(n_heads,),
        in_specs=[pl.BlockSpec(memory_space=pltpu.SMEM),
                  pl.BlockSpec((2, MOBA_BLOCK, MOBA_BLOCK), lambda h: (0, 0, 0))],
        out_specs=pl.BlockSpec((None, 2, MOBA_BLOCK, MOBA_BLOCK), lambda h: (h, 0, 0, 0)),
        out_shape=jax.ShapeDtypeStruct((n_heads, 2, MOBA_BLOCK, MOBA_BLOCK), F32),
        compiler_params=_cparams(1),
        name="bias_tiles",
    )(rel_bias, idx)


def _moba_kernel(q_ref, k_ref, v_ref, bias_ref, far_ref, o_ref,
                 kb_ref, vt_ref, km_ref, sel_ref, *, nblk, scale):
    h = pl.program_id(0)
    i = pl.program_id(2)
    blk = MOBA_BLOCK

    @pl.when(i == 0)
    def _prep():
        km_ref[...] = jnp.zeros(km_ref.shape, F32)
        for n in range(nblk):
            kk = k_ref[n * blk:(n + 1) * blk, :]
            km_ref[n:n + 1, :] = jnp.sum(kk, axis=0, keepdims=True) * (1.0 / blk)
            kb_ref[n] = kk.astype(BF16)
            vt_ref[n] = v_ref[n * blk:(n + 1) * blk, :].T.astype(BF16)

    qt = q_ref[...].T
    gate = _dot3(km_ref[...], qt)
    n_iota = lax.broadcasted_iota(I32, gate.shape, 0)
    cnt = jnp.zeros(gate.shape, I32)
    for n2 in range(nblk):
        g2 = gate[n2:n2 + 1, :]
        beats = (g2 > gate) | ((g2 == gate) & (n2 < n_iota))
        cnt = cnt + jnp.where(beats, 1, 0) * (n2 < i).astype(I32)
    sel = (n_iota < i) & (cnt < MOBA_TOPK)
    sel_ref[...] = jnp.where(sel, 1.0, 0.0)

    qs = (qt * scale).astype(BF16)
    k_iota = lax.broadcasted_iota(I32, (blk, blk), 0)
    q_iota = lax.broadcasted_iota(I32, (blk, blk), 1)
    st = _dot(kb_ref[i], qs) + bias_ref[0]
    st = jnp.where(k_iota <= q_iota, st, NEG_BIG)
    m0 = jnp.max(st, axis=0, keepdims=True)
    p0 = jnp.exp(st - m0)
    l0 = jnp.sum(p0, axis=0, keepdims=True)
    acc0 = _dot(vt_ref[i], p0.astype(BF16))
    far = far_ref[h]

    def past_block(j, carry):
        m, l, acc = carry
        is_prev = jnp.full((blk, blk), j, I32) == (i - 1)
        bias = jnp.where(is_prev, bias_ref[1], far)
        s = _dot(kb_ref[j], qs) + bias
        s = jnp.where(sel_ref[pl.ds(j, 1), :] > 0.5, s, NEG_BIG)
        m_new = jnp.maximum(m, jnp.max(s, axis=0, keepdims=True))
        a = jnp.exp(m - m_new)
        p = jnp.exp(s - m_new)
        l = a * l + jnp.sum(p, axis=0, keepdims=True)
        acc = a * acc + _dot(vt_ref[j], p.astype(BF16))
        return m_new, l, acc

    _, l, acc = lax.fori_loop(0, i, past_block, (m0, l0, acc0))
    o_ref[...] = (acc / l).T.astype(o_ref.dtype)


def _moba_prompt(q_all, k_all, v_all, bias_t, far_bias, n_batch, seq, n_heads, dh):
    nblk = seq // MOBA_BLOCK
    blk = MOBA_BLOCK
    kern = functools.partial(_moba_kernel, nblk=nblk, scale=dh ** -0.5)
    return pl.pallas_call(
        kern,
        grid=(n_heads, n_batch, nblk),
        in_specs=[
            pl.BlockSpec((blk, dh), lambda h, b, i: (b * nblk + i, h)),
            pl.BlockSpec((seq, dh), lambda h, b, i: (b, h)),
            pl.BlockSpec((seq, dh), lambda h, b, i: (b, h)),
            pl.BlockSpec((None, 2, blk, blk), lambda h, b, i: (h, 0, 0, 0)),
            pl.BlockSpec(memory_space=pltpu.SMEM),
        ],
        out_specs=pl.BlockSpec((blk, dh), lambda h, b, i: (b * nblk + i, h)),
        out_shape=jax.ShapeDtypeStruct((n_batch * seq, n_heads * dh), BF16),
        scratch_shapes=[
            pltpu.VMEM((nblk, blk, dh), BF16),
            pltpu.VMEM((nblk, dh, blk), BF16),
            pltpu.VMEM((max(nblk, BF16_ROWS), dh), F32),
            pltpu.VMEM((max(nblk, BF16_ROWS), blk), F32),
        ],
        compiler_params=_cparams(3),
        name="moba_prompt",
    )(q_all, k_all, v_all, bias_t, far_bias)


def _kmean_kernel(pt_ref, k_ref, o_ref, *, inv_blk):
    p = pl.program_id(2)
    s = jnp.sum(k_ref[...], axis=0)

    @pl.when(p % 2 == 0)
    def _first():
        o_ref[...] = s

    @pl.when(p % 2 == 1)
    def _second():
        o_ref[...] = (o_ref[...] + s) * inv_blk


def _cache_kmean(cache_k, page_table):
    depth, _, page, n_heads, dh = cache_k.shape
    n_batch, n_pages = page_table.shape
    assert MOBA_BLOCK == 2 * page
    kern = functools.partial(_kmean_kernel, inv_blk=1.0 / MOBA_BLOCK)
    return pl.pallas_call(
        kern,
        grid_spec=pltpu.PrefetchScalarGridSpec(
            num_scalar_prefetch=1,
            grid=(depth, n_batch, n_pages),
            in_specs=[pl.BlockSpec((None, None, page, n_heads, dh),
                                   lambda l, b, p, pt: (l, pt[b, p], 0, 0, 0))],
            out_specs=pl.BlockSpec((None, None, None, n_heads, dh),
                                   lambda l, b, p, pt: (l, b, p // 2, 0, 0)),
        ),
        out_shape=jax.ShapeDtypeStruct((depth, n_batch, n_pages // 2, n_heads, dh), F32),
        compiler_params=_cparams(3),
        name="cache_kmean",
    )(page_table, cache_k)


def _gate_rank_kernel(qt_ref, km_ref, o_ref, gate_ref):
    nblk = km_ref.shape[0]
    qt = qt_ref[...]
    for n in range(nblk):
        gate_ref[n:n + 1, :] = jnp.sum(km_ref[n] * qt, axis=0, keepdims=True)
    gate = gate_ref[...]
    n_iota = lax.broadcasted_iota(I32, gate.shape, 0)
    cnt = jnp.zeros(gate.shape, I32)
    for n2 in range(nblk):
        g2 = gate[n2:n2 + 1, :]
        beats = (g2 > gate) | ((g2 == gate) & (n2 < n_iota))
        cnt = cnt + jnp.where(beats, 1, 0)
    o_ref[...] = cnt


def _sample_gate_rank(q_t, kmean_t):
    nblk, dh, bh = kmean_t.shape
    return pl.pallas_call(
        _gate_rank_kernel,
        grid=(1,),
        in_specs=[pl.BlockSpec((dh, bh), lambda g: (0, 0)),
                  pl.BlockSpec((nblk, dh, bh), lambda g: (0, 0, 0))],
        out_specs=pl.BlockSpec((nblk, bh), lambda g: (0, 0)),
        out_shape=jax.ShapeDtypeStruct((nblk, bh), I32),
        scratch_shapes=[pltpu.VMEM((nblk, bh), F32)],
        compiler_params=_cparams(1),
        name="sample_gate_rank",
    )(q_t, kmean_t)


def _sample_attend_kernel(pg_ref, bk_ref, q_ref, kn_ref, vn_ref, prev_ref, far_ref, own_ref,
                          *rest, n_slab, n_heads, last_blk, scale):
    k_refs = rest[:n_slab]
    v_refs = rest[n_slab:2 * n_slab]
    o_ref = rest[2 * n_slab]
    b = pl.program_id(0)
    h = pl.program_id(1)
    q = q_ref[...]
    page = k_refs[0].shape[0]
    far = far_ref[h]
    s_own = jnp.sum(q * kn_ref[...], axis=-1, keepdims=True) * scale + own_ref[h]
    scores = []
    m = s_own
    for s in range(n_slab):
        blk_id = bk_ref[(b * n_heads + h) * (n_slab // 2) + s // 2]
        half = s % 2
        near = prev_ref[half * page:(half + 1) * page, :]
        is_last = jnp.full(near.shape, blk_id, I32) == last_blk
        bias = jnp.where(is_last, near, far)
        sc = jnp.sum(k_refs[s][...] * q, axis=-1, keepdims=True) * scale + bias
        scores.append(sc)
        m = jnp.maximum(m, jnp.max(sc, axis=0, keepdims=True))
    p_own = jnp.exp(s_own - m)
    l = p_own
    acc = p_own * vn_ref[...]
    for s in range(n_slab):
        p = jnp.exp(scores[s] - m)
        l = l + jnp.sum(p, axis=0, keepdims=True)
        acc = acc + jnp.sum(p * v_refs[s][...], axis=0, keepdims=True)
    o_ref[...] = acc / l


def _sample_attend(layer, q_s, k_new, v_new, cache_k, cache_v, pages, blks, prev_col,
                   far_bias, own_bias, last_blk):
    n_batch, n_heads, dh = q_s.shape
    page = cache_k.shape[2]
    n_slab = pages.shape[0] // (n_batch * n_heads)
    vec = lambda a: a.reshape(n_batch, n_heads, 1, dh)
    vec_spec = pl.BlockSpec((None, None, 1, dh), lambda b, h, pg, bk: (b, h, 0, 0))

    flat = lambda c: c.reshape(c.shape[0], c.shape[1], page, n_heads * dh)
    cache_k, cache_v = flat(cache_k), flat(cache_v)

    def slab_spec(s):
        return pl.BlockSpec(
            (None, None, page, dh),
            lambda b, h, pg, bk, s=s: (layer, pg[(b * n_heads + h) * n_slab + s], 0, h))

    kern = functools.partial(_sample_attend_kernel, n_slab=n_slab, n_heads=n_heads,
                             last_blk=last_blk, scale=dh ** -0.5)
    out = pl.pallas_call(
        kern,
        grid_spec=pltpu.PrefetchScalarGridSpec(
            num_scalar_prefetch=2,
            grid=(n_batch, n_heads),
            in_specs=[vec_spec, vec_spec, vec_spec,
                      pl.BlockSpec((None, MOBA_BLOCK, 1), lambda b, h, pg, bk: (h, 0, 0)),
                      pl.BlockSpec(memory_space=pltpu.SMEM),
                      pl.BlockSpec(memory_space=pltpu.SMEM)]
                     + [slab_spec(s) for s in range(n_slab)]
                     + [slab_spec(s) for s in range(n_slab)],
            out_specs=vec_spec,
        ),
        out_shape=jax.ShapeDtypeStruct((n_batch, n_heads, 1, dh), F32),
        compiler_params=_cparams(2),
        name="sample_attend",
    )(pages, blks, vec(q_s), vec(k_new), vec(v_new), prev_col, far_bias, own_bias,
      *([cache_k] * n_slab), *([cache_v] * n_slab))
    return out.reshape(n_batch, n_heads * dh)


def _lower_bound(lb_ref, layer):
    lg = lb_ref[...]
    e = jnp.exp(lg - jnp.max(lg, axis=0, keepdims=True))
    sm = e / jnp.sum(e, axis=0, keepdims=True)
    low = sm[0]
    for j in range(1, layer + 1):
        low = low + sm[j]
    return low - sm[0]


def _cumsum_rows(x):
    n = x.shape[0]
    r = lax.broadcasted_iota(I32, x.shape, 0)
    d = 1
    while d < n:
        x = x + jnp.where(r >= d, pltpu.roll(x, d, axis=0), 0.0)
        d *= 2
    return x


def _hgrn_kernel(q_ref, f_ref, i_ref, g_ref, lb_ref, gr_ref, o_ref, s_ref, *, layer, seq):
    c_len = HGRN_CHUNK
    sub = 8
    lb = _lower_bound(lb_ref, layer)
    dk = q_ref.shape[1]
    dv = i_ref.shape[1]
    t_iota = lax.broadcasted_iota(I32, (sub, dk), 0)

    def chunk(c, st):
        r0 = pl.multiple_of(c * c_len, c_len)
        rows = pl.ds(r0, c_len)
        q = q_ref[rows, :]
        z = f_ref[rows, :]
        v = i_ref[rows, :]
        f = lb + (1.0 - lb) * _sigmoid(z)
        kk = (1.0 - lb) * _sigmoid(-z)
        b = _cumsum_rows(jnp.log(jnp.maximum(f, TINY)))
        b_end = b[c_len - 1:c_len, :]
        o = lax.dot_general((q * jnp.exp(b)).astype(BF16), st.astype(BF16),
                            (((1,), (1,)), ((), ())), preferred_element_type=F32)
        o_blocks = []
        for blk_i in range(c_len // sub):
            lo = blk_i * sub
            q_i = q[lo:lo + sub, :]
            b_i = b[lo:lo + sub, :]
            o_i = o[lo:lo + sub, :]
            for s in range(lo + sub):
                d = b_i - b[s:s + 1, :]
                if s >= lo:
                    d = jnp.where(t_iota >= s - lo, d, NEG_BIG)
                w = q_i * kk[s:s + 1, :] * jnp.exp(d)
                a = jnp.sum(w, axis=-1, keepdims=True)
                o_i = o_i + a * v[s:s + 1, :]
            o_blocks.append(o_i)
        o = jnp.concatenate(o_blocks, axis=0)
        kd = kk * jnp.exp(b_end - b)
        st = st * jnp.exp(b_end) + lax.dot_general(
            v.astype(BF16), kd.astype(BF16), (((0,), (0,)), ((), ())),
            preferred_element_type=F32)
        ms = jnp.mean(o * o, axis=-1, keepdims=True)
        y = o * lax.rsqrt(ms + EPS) * gr_ref[...]
        o_ref[rows, :] = (y * _silu(g_ref[rows, :])).astype(o_ref.dtype)
        return st

    st = lax.fori_loop(0, seq // c_len, chunk, jnp.zeros((dv, dk), F32))
    s_ref[...] = st.T


def _hgrn_prompt(r_all, lb_logits, g_rout, layer, n_batch, seq, n_heads, dk):
    depth = lb_logits.shape[0]
    assert seq % HGRN_CHUNK == 0
    kern = functools.partial(_hgrn_kernel, layer=layer, seq=seq)
    seg = lambda s: pl.BlockSpec((seq, dk), lambda b, h, s=s: (b, s * n_heads + h))
    return pl.pallas_call(
        kern,
        grid=(n_batch, n_heads),
        in_specs=[seg(0), seg(1), seg(2), seg(3),
                  pl.BlockSpec((depth, 1, dk), lambda b, h: (0, 0, h)),
                  pl.BlockSpec((1, dk), lambda b, h: (0, 0))],
        out_specs=[pl.BlockSpec((seq, dk), lambda b, h: (b, h)),
                   pl.BlockSpec((None, None, dk, dk), lambda b, h: (b, h, 0, 0))],
        out_shape=[jax.ShapeDtypeStruct((n_batch * seq, n_heads * dk), BF16),
                   jax.ShapeDtypeStruct((n_batch, n_heads, dk, dk), F32)],
        compiler_params=_cparams(2),
        name="hgrn_prompt",
    )(r_all, r_all, r_all, r_all, lb_logits.reshape(depth, 1, -1), g_rout.reshape(1, dk))


def _hgrn_step_kernel(q_ref, z_ref, v_ref, g_ref, lb_ref, gr_ref, s_ref, o_ref, sn_ref, *, layer):
    lb = _lower_bound(lb_ref, layer)
    z = z_ref[...]
    f = lb + (1.0 - lb) * _sigmoid(z)
    kk = (1.0 - lb) * _sigmoid(-z)
    dec = jnp.exp(jnp.log(jnp.maximum(f, TINY)))
    s_new = dec * s_ref[...] + kk * v_ref[...]
    sn_ref[...] = s_new
    o = jnp.sum(q_ref[...] * s_new, axis=1, keepdims=True)
    ms = jnp.mean(o * o, axis=-1, keepdims=True)
    y = o * lax.rsqrt(ms + EPS) * gr_ref[...]
    o_ref[...] = y * _silu(g_ref[...])


def _hgrn_sample(r_s, lb_logits, g_rout, state, layer, n_heads, dk):
    n_batch = r_s.shape[0]
    depth = lb_logits.shape[0]
    width = n_heads * dk
    col = lambda s: r_s[:, s * width:(s + 1) * width].reshape(n_batch, n_heads, dk, 1)
    row = lambda s: r_s[:, s * width:(s + 1) * width].reshape(n_batch, n_heads, 1, dk)
    col_spec = pl.BlockSpec((None, n_heads, dk, 1), lambda b: (b, 0, 0, 0))
    row_spec = pl.BlockSpec((None, n_heads, 1, dk), lambda b: (b, 0, 0, 0))
    kern = functools.partial(_hgrn_step_kernel, layer=layer)
    o, s_new = pl.pallas_call(
        kern,
        grid=(n_batch,),
        in_specs=[col_spec, col_spec, row_spec, row_spec,
                  pl.BlockSpec((depth, n_heads, dk, 1), lambda b: (0, 0, 0, 0)),
                  pl.BlockSpec((1, 1, dk), lambda b: (0, 0, 0)),
                  pl.BlockSpec((None, None, n_heads, dk, dk), lambda b: (layer, b, 0, 0, 0))],
        out_specs=[row_spec,
                   pl.BlockSpec((None, n_heads, dk, dk), lambda b: (b, 0, 0, 0))],
        out_shape=[jax.ShapeDtypeStruct((n_batch, n_heads, 1, dk), F32),
                   jax.ShapeDtypeStruct((n_batch, n_heads, dk, dk), F32)],
        compiler_params=_cparams(1),
        name="hgrn_sample",
    )(col(0), col(1), row(2), row(3), lb_logits.reshape(depth, n_heads, dk, 1),
      g_rout.reshape(1, 1, dk), state)
    return o.reshape(n_batch, width), s_new


ROUTE_ID0 = N_GROUPS + N_EXPERTS
ROUTE_WT0 = ROUTE_ID0 + MOE_TOPK


def _first_lane(mask, lane):
    return jnp.min(jnp.where(mask, lane, float(LANES)), axis=-1, keepdims=True)


def _router_kernel(h_ref, w_ref, b_ref, o_ref):
    logit = _dot3(h_ref[...], w_ref[...]) + b_ref[...]
    lane = lax.broadcasted_iota(I32, logit.shape, 1).astype(F32)
    neg_inf = -jnp.inf
    in_grp = lane < N_GROUPS
    gl = jnp.where(in_grp, logit, neg_inf)
    gmax = jnp.max(gl, axis=-1, keepdims=True)
    gsum = jnp.sum(jnp.where(in_grp, jnp.exp(logit - gmax), 0.0), axis=-1, keepdims=True)
    g_val = 1.0 / gsum
    g_idx = _first_lane(gl == gmax, lane)
    e0 = N_GROUPS + g_idx * EXPERTS_PER_GROUP
    in_exp = (lane >= e0) & (lane < e0 + EXPERTS_PER_GROUP)
    el = jnp.where(in_exp, logit, neg_inf)
    emax = jnp.max(el, axis=-1, keepdims=True)
    i1 = _first_lane(el == emax, lane)
    el2 = jnp.where(lane == i1, neg_inf, el)
    emax2 = jnp.max(el2, axis=-1, keepdims=True)
    i2 = _first_lane(el2 == emax2, lane)
    p1 = 1.0
    p2 = jnp.exp(emax2 - emax)
    w1 = g_val * p1 / (p1 + p2)
    w2 = g_val * p2 / (p1 + p2)
    out = jnp.where(lane == ROUTE_ID0, i1 - N_GROUPS, 0.0)
    out = jnp.where(lane == ROUTE_ID0 + 1, i2 - N_GROUPS, out)
    out = jnp.where(lane == ROUTE_WT0, w1, out)
    out = jnp.where(lane == ROUTE_WT0 + 1, w2, out)
    o_ref[...] = out


def _router(h2, w_grp, b_grp, w_erout, b_erout):
    m, d = h2.shape
    pad = LANES - N_GROUPS - N_EXPERTS
    w = jnp.concatenate([w_grp, w_erout, jnp.zeros((d, pad), F32)], axis=1)
    b = jnp.concatenate([b_grp, b_erout, jnp.zeros((pad,), F32)]).reshape(1, LANES)
    tm = _row_tile(m, 320)
    return pl.pallas_call(
        _router_kernel,
        grid=(m // tm,),
        in_specs=[pl.BlockSpec((tm, d), lambda i: (i, 0)),
                  pl.BlockSpec((d, LANES), lambda i: (0, 0)),
                  pl.BlockSpec((1, LANES), lambda i: (0, 0))],
        out_specs=pl.BlockSpec((tm, LANES), lambda i: (i, 0)),
        out_shape=jax.ShapeDtypeStruct((m, LANES), F32),
        compiler_params=_cparams(1),
        name="router",
    )(h2, w, b)


def _route_plan(ids, wts, n_tiles):
    n_tok = ids.shape[0]
    e = ids.reshape(-1)
    w = wts.reshape(-1)
    tok = jnp.repeat(jnp.arange(n_tok, dtype=I32), MOE_TOPK)
    onehot = (e[:, None] == jnp.arange(N_EXPERTS, dtype=I32)[None, :]).astype(I32)
    csum = jnp.cumsum(onehot, axis=0)
    rank = jnp.take_along_axis(csum, e[:, None], axis=1)[:, 0] - 1
    counts = csum[-1]
    padded = ((counts + MOE_ROWS - 1) // MOE_ROWS) * MOE_ROWS
    ends = jnp.cumsum(padded)
    dest = (ends - padded)[e] + rank
    rows = n_tiles * MOE_ROWS
    row_tok = jnp.zeros((rows,), I32).at[dest].set(tok)
    row_wt = jnp.zeros((rows,), F32).at[dest].set(w)
    tile_start = jnp.arange(n_tiles, dtype=I32) * MOE_ROWS
    tile_e = jnp.minimum(jnp.searchsorted(ends, tile_start, side="right"),
                         N_EXPERTS - 1).astype(I32)
    tile_ok = (tile_start < ends[-1]).astype(I32)
    return row_tok, row_wt.reshape(rows, 1), tile_e, tile_ok, dest.astype(I32)


def _row_copy(src_hbm, row, dst, r, sem):
    return pltpu.make_async_copy(src_hbm.at[pl.ds(row, 1), :], dst.at[pl.ds(r, 1), :], sem)


def _gather_kernel(idx_ref, src_hbm, o_ref, buf, sem):
    n = buf.shape[0]
    base = pl.program_id(0) * n

    def issue(r, c):
        _row_copy(src_hbm, idx_ref[base + r], buf, r, sem).start()
        return c

    def wait(r, c):
        _row_copy(src_hbm, 0, buf, r, sem).wait()
        return c

    lax.fori_loop(0, n, issue, 0)
    lax.fori_loop(0, n, wait, 0)
    o_ref[...] = buf[...].astype(o_ref.dtype)


def _gather_rows(src, row_idx, out_dtype):
    rows = row_idx.shape[0]
    d = src.shape[1]
    return pl.pallas_call(
        _gather_kernel,
        grid_spec=pltpu.PrefetchScalarGridSpec(
            num_scalar_prefetch=1,
            grid=(rows // MOE_ROWS,),
            in_specs=[pl.BlockSpec(memory_space=pl.ANY)],
            out_specs=pl.BlockSpec((MOE_ROWS, d), lambda t, idx: (t, 0)),
            scratch_shapes=[pltpu.VMEM((MOE_ROWS, d), src.dtype),
                            pltpu.SemaphoreType.DMA(())],
        ),
        out_shape=jax.ShapeDtypeStruct((rows, d), out_dtype),
        compiler_params=_cparams(1),
        name="moe_gather",
    )(row_idx, src)


def _tile_changed(te_ref, t):
    prev = te_ref[jnp.maximum(t - 1, 0)]
    return (t == 0) | (te_ref[t] != prev)


def _expert_up_kernel(te_ref, ok_ref, x_ref, wg_ref, wu_ref, wt_ref, o_ref, wgb, wub):
    t = pl.program_id(1)

    @pl.when(_tile_changed(te_ref, t))
    def _cast_weights():
        wgb[...] = wg_ref[...].astype(BF16)
        wub[...] = wu_ref[...].astype(BF16)

    @pl.when(ok_ref[t] > 0)
    def _compute():
        x = x_ref[...]
        a = _dot(x, wgb[...])
        u = _dot(x, wub[...])
        o_ref[...] = (_silu(a) * u * wt_ref[...]).astype(o_ref.dtype)

    @pl.when(ok_ref[t] == 0)
    def _skip():
        o_ref[...] = jnp.zeros(o_ref.shape, o_ref.dtype)


def _expert_up(xg, row_wt, tile_e, tile_ok, w_gate, w_up, layer, tf):
    rows, d = xg.shape
    f = w_gate.shape[-1]
    n_tiles = rows // MOE_ROWS
    w_spec = pl.BlockSpec((None, None, d, tf), lambda c, t, te, ok: (layer, te[t], 0, c))
    return pl.pallas_call(
        _expert_up_kernel,
        grid_spec=pltpu.PrefetchScalarGridSpec(
            num_scalar_prefetch=2,
            grid=(f // tf, n_tiles),
            in_specs=[pl.BlockSpec((MOE_ROWS, d), lambda c, t, te, ok: (t, 0)),
                      w_spec, w_spec,
                      pl.BlockSpec((MOE_ROWS, 1), lambda c, t, te, ok: (t, 0))],
            out_specs=pl.BlockSpec((MOE_ROWS, tf), lambda c, t, te, ok: (t, c)),
            scratch_shapes=[pltpu.VMEM((d, tf), BF16), pltpu.VMEM((d, tf), BF16)],
        ),
        out_shape=jax.ShapeDtypeStruct((rows, f), BF16),
        compiler_params=_cparams(2),
        name="expert_up",
    )(tile_e, tile_ok, xg, w_gate, w_up, row_wt)


def _expert_down_kernel(te_ref, ok_ref, h_ref, wd_ref, o_ref, wdb):
    t = pl.program_id(1)

    @pl.when(_tile_changed(te_ref, t))
    def _cast_weights():
        wdb[...] = wd_ref[...].astype(BF16)

    @pl.when(ok_ref[t] > 0)
    def _compute():
        o_ref[...] = _dot(h_ref[...], wdb[...])

    @pl.when(ok_ref[t] == 0)
    def _skip():
        o_ref[...] = jnp.zeros(o_ref.shape, o_ref.dtype)


def _expert_down(hid, tile_e, tile_ok, w_down, layer, tn):
    rows, f = hid.shape
    d = w_down.shape[-1]
    n_tiles = rows // MOE_ROWS
    return pl.pallas_call(
        _expert_down_kernel,
        grid_spec=pltpu.PrefetchScalarGridSpec(
            num_scalar_prefetch=2,
            grid=(d // tn, n_tiles),
            in_specs=[pl.BlockSpec((MOE_ROWS, f), lambda n, t, te, ok: (t, 0)),
                      pl.BlockSpec((None, None, f, tn),
                                   lambda n, t, te, ok: (layer, te[t], 0, n))],
            out_specs=pl.BlockSpec((MOE_ROWS, tn), lambda n, t, te, ok: (t, n)),
            scratch_shapes=[pltpu.VMEM((f, tn), BF16)],
        ),
        out_shape=jax.ShapeDtypeStruct((rows, d), F32),
        compiler_params=_cparams(2),
        name="expert_down",
    )(tile_e, tile_ok, hid, w_down)


def _combine_kernel(pos_ref, x_ref, ys_hbm, o_ref, buf, sem):
    n = x_ref.shape[0]
    base = pl.program_id(0) * n

    def issue(r, c):
        for k in range(MOE_TOPK):
            _row_copy(ys_hbm, pos_ref[(base + r) * MOE_TOPK + k], buf.at[k], r, sem).start()
        return c

    def wait(r, c):
        for k in range(MOE_TOPK):
            _row_copy(ys_hbm, 0, buf.at[k], r, sem).wait()
        return c

    lax.fori_loop(0, n, issue, 0)
    lax.fori_loop(0, n, wait, 0)
    y = buf[0]
    for k in range(1, MOE_TOPK):
        y = y + buf[k]
    o_ref[...] = x_ref[...] + y


def _combine(x, y_sorted, pos):
    m, d = x.shape
    tm = _row_tile(m, 320)
    return pl.pallas_call(
        _combine_kernel,
        grid_spec=pltpu.PrefetchScalarGridSpec(
            num_scalar_prefetch=1,
            grid=(m // tm,),
            in_specs=[pl.BlockSpec((tm, d), lambda i, pos: (i, 0)),
                      pl.BlockSpec(memory_space=pl.ANY)],
            out_specs=pl.BlockSpec((tm, d), lambda i, pos: (i, 0)),
            scratch_shapes=[pltpu.VMEM((MOE_TOPK, tm, d), F32),
                            pltpu.SemaphoreType.DMA(())],
        ),
        out_shape=jax.ShapeDtypeStruct((m, d), F32),
        compiler_params=_cparams(1),
        name="moe_combine",
    )(pos, x, y_sorted)


def _hier_moe(x1, n_real, g_ffn, w_grp, b_grp, w_erout, b_erout, w_gate, w_up, w_down, layer):
    m, _ = x1.shape
    h2 = _rmsnorm(x1, g_ffn, F32)
    route = _router(h2, w_grp, b_grp, w_erout, b_erout)
    ids = route[:n_real, ROUTE_ID0:ROUTE_ID0 + MOE_TOPK].astype(I32)
    wts = route[:n_real, ROUTE_WT0:ROUTE_WT0 + MOE_TOPK]
    n_tiles = -(-n_real * MOE_TOPK // MOE_ROWS) + N_EXPERTS
    row_tok, row_wt, tile_e, tile_ok, dest = _route_plan(ids, wts, n_tiles)
    xg = _gather_rows(h2, row_tok, BF16)
    hid = _expert_up(xg, row_wt, tile_e, tile_ok, w_gate, w_up, layer,
                     tf=min(512, w_gate.shape[-1]))
    y_sorted = _expert_down(hid, tile_e, tile_ok, w_down, layer, tn=min(1024, w_down.shape[-1]))
    pos = jnp.zeros((m * MOE_TOPK,), I32).at[:n_real * MOE_TOPK].set(dest)
    return _combine(x1, y_sorted, pos)


def kernel(x_prompt, x_sample, cache_k, cache_v, state_hgrn, page_table, p_prompt, p_sample,
           rel_bias, lb_logits, g_mix, w_in, g_q, g_k, g_rout, w_ba, w_br, w_out, g_ffn,
           w_grp, b_grp, w_erout, b_erout, w_gate, w_up, w_down, g_ple, w_pg, w_pp):
    bp, lp, d = x_prompt.shape
    bs, ls, _ = x_sample.shape
    depth = w_in.shape[0]
    dh = g_q.shape[1]
    a_width = w_ba.shape[1]
    n_heads = a_width // dh
    dk = g_rout.shape[1]
    r_width = w_br.shape[1]
    n_heads_r = r_width // dk
    page = cache_k.shape[2]
    assert ls == 1 and lp % MOBA_BLOCK == 0
    mp, ms = bp * lp, bs * ls
    n_real = mp + ms
    m_all = -(-n_real // BF16_ROWS) * BF16_ROWS
    past = page_table.shape[1] * page
    assert past % MOBA_BLOCK == 0
    n_past_blk = past // MOBA_BLOCK
    n_sel = min(MOBA_TOPK, n_past_blk)

    pad_rows = lambda a: jnp.concatenate(
        [a, jnp.zeros((m_all - n_real,) + a.shape[1:], a.dtype)], axis=0)
    x = pad_rows(jnp.concatenate([x_prompt.reshape(mp, d), x_sample.reshape(ms, d)], axis=0))

    assert MAX_DISTANCE <= MOBA_BLOCK
    bias_t = _bias_tiles(rel_bias)
    far_bias = rel_bias[N_BUCKETS - 1]
    own_bias = rel_bias[0]
    prev_col = bias_t[:, 1, :, 0:1]
    kmean_s = _cache_kmean(cache_k, page_table)
    kmean_t = kmean_s.transpose(0, 2, 4, 1, 3).reshape(depth, n_past_blk, dh, bs * n_heads)

    c_q, c_k, c_v = 0, a_width, 2 * a_width
    c_r = 3 * a_width
    c_g = c_r + 4 * r_width

    kps, vps, sps, kss, vss, sss = [], [], [], [], [], []
    for i in range(depth):
        h = _rmsnorm(x, g_mix[i], BF16)
        win = (w_in, (i,))
        gq = jnp.tile(g_q[i], n_heads).reshape(1, a_width)
        gk = jnp.tile(g_k[i], n_heads).reshape(1, a_width)
        q_all = _fused_matmul([h], [win + (c_q,)], [(gq, 0)], _ep_headnorm, F32, a_width, 512, "proj_q")
        k_all = _fused_matmul([h], [win + (c_k,)], [(gk, 0)], _ep_headnorm, F32, a_width, 512, "proj_k")
        v_all = _fused_matmul([h], [win + (c_v,)], [], _ep_plain, F32, a_width, 512, "proj_v")
        r_all = _fused_matmul([h], [win + (c_r,)], [], _ep_plain, F32, 4 * r_width, 512, "proj_r")
        gates = _fused_matmul([h], [win + (c_g,)], [], _ep_sigmoid, BF16, 2 * d, 512, "proj_gate")

        o_a_p = _moba_prompt(q_all, k_all, v_all, bias_t, far_bias, bp, lp, n_heads, dh)
        q_s = q_all[mp:n_real].reshape(ms, n_heads, dh)
        k_s = k_all[mp:n_real].reshape(ms, n_heads, dh)
        v_s = v_all[mp:n_real].reshape(ms, n_heads, dh)
        rank = _sample_gate_rank(q_s.reshape(ms * n_heads, dh).T, kmean_t[i])
        blks = jnp.argsort(rank, axis=0)[:n_sel].T.reshape(bs, n_heads, n_sel).astype(I32)
        pages = jnp.stack([page_table[jnp.arange(bs)[:, None, None], 2 * blks],
                           page_table[jnp.arange(bs)[:, None, None], 2 * blks + 1]], axis=-1)
        o_a_s = _sample_attend(i, q_s, k_s, v_s, cache_k, cache_v, pages.reshape(-1),
                               blks.reshape(-1), prev_col, far_bias, own_bias, n_past_blk - 1)
        o_a = pad_rows(jnp.concatenate([o_a_p, o_a_s.astype(BF16)], axis=0))

        o_r_p, s_p = _hgrn_prompt(r_all, lb_logits, g_rout[i], i, bp, lp, n_heads_r, dk)
        o_r_s, s_s = _hgrn_sample(r_all[mp:n_real], lb_logits, g_rout[i], state_hgrn, i,
                                  n_heads_r, dk)
        o_r = pad_rows(jnp.concatenate([o_r_p, o_r_s.astype(BF16)], axis=0))

        merged = _fused_matmul([o_a, o_r], [(w_ba, (i,), 0), (w_br, (i,), 0)],
                               [(gates, 0), (gates, d)], _ep_merge, BF16, d, 512, "merge")
        x1 = _fused_matmul([merged], [(w_out, (i,), 0)], [(x, 0)], _ep_residual, F32, d, 512, "out_proj")
        x2 = _hier_moe(x1, n_real, g_ffn[i], w_grp[i], b_grp[i], w_erout[i], b_erout[i],
                       w_gate, w_up, w_down, i)
        h3 = _rmsnorm(x2, g_ple[i], BF16)
        p_all = pad_rows(jnp.concatenate([p_prompt[i].reshape(mp, -1),
                                          p_sample[i].reshape(ms, -1)], axis=0)).astype(BF16)
        x = _fused_matmul([h3, p_all], [(w_pg, (i,), 0), (w_pp, (i,), 0)], [(x2, 0)], _ep_ple,
                          F32, d, 512, "ple")

        kps.append(k_all[:mp].reshape(bp, lp, n_heads, dh))
        vps.append(v_all[:mp].reshape(bp, lp, n_heads, dh))
        sps.append(s_p)
        kss.append(k_s.reshape(bs, ls, n_heads, dh))
        vss.append(v_s.reshape(bs, ls, n_heads, dh))
        sss.append(s_s)

    return (x[:mp].reshape(bp, lp, d), x[mp:n_real].reshape(bs, ls, d),
            jnp.stack(kps), jnp.stack(vps), jnp.stack(sps),
            jnp.stack(kss), jnp.stack(vss), jnp.stack(sss))
```
